```python
import math
import jax, jax.numpy as jnp
from jax import lax
import numpy as np

D_MODEL = 1024
BATCH = 8
SEQ = 2048
DEPTH = 2
DEC_BATCH = 128
DEC_SEQ = 1
PAST_LEN = 8192
PAGE_SIZE = 128

D_MIX = D_MODEL
N_GROUPS = 4
G_WIDTH = D_MIX // N_GROUPS
POOL_WINDOWS = (2, 4, 8, 16)
POOL_G = len(POOL_WINDOWS)
POOL_CG = G_WIDTH // POOL_G
POOL_HIST = max(POOL_WINDOWS) - 1
GM_HEADS = 4
GM_DH = G_WIDTH // GM_HEADS
CHUNK = 128
SB_HEADS = 4
SB_DH = G_WIDTH // SB_HEADS
MLA_HEADS = 4
MLA_NOPE = 64
MLA_ROPE = 32
MLA_V = G_WIDTH // MLA_HEADS
MLA_Q_RANK = D_MODEL // 4
MLA_KV_RANK = D_MODEL // 8
MLA_SCALE = 1.0 / math.sqrt(MLA_NOPE + MLA_ROPE)
ROPE_THETA = 10000.0
IN_SIZES = (G_WIDTH, G_WIDTH, G_WIDTH, G_WIDTH, G_WIDTH, G_WIDTH, MLA_Q_RANK, MLA_KV_RANK, MLA_ROPE)
D_IN = 6 * G_WIDTH + MLA_Q_RANK + MLA_KV_RANK + MLA_ROPE
D_FF = ((8 * D_MODEL // 3 + 127) // 128) * 128
CONV_W = 3
Q_BLOCK = 128
EPS = 1e-6

kernel_name = "hymba_style_pool_gmlp_stickbreak_mla_decoder_step"


def rmsnorm(x, g):
    xf = x.astype(jnp.float32)
    y = xf * lax.rsqrt(jnp.mean(xf * xf, axis=-1, keepdims=True) + EPS)
    return (y * g).astype(x.dtype)


def layernorm(x, g, b):
    xf = x.astype(jnp.float32)
    mu = jnp.mean(xf, axis=-1, keepdims=True)
    xc = xf - mu
    y = xc * lax.rsqrt(jnp.mean(xc * xc, axis=-1, keepdims=True) + EPS)
    return (y * g + b).astype(x.dtype)


def rope(x, pos):
    half = x.shape[-1] // 2
    freq = ROPE_THETA ** (-jnp.arange(half, dtype=jnp.float32) / half)
    ang = pos.astype(jnp.float32)[:, None] * freq
    ang = ang.reshape((1, pos.shape[0]) + (1,) * (x.ndim - 3) + (half,))
    cos, sin = jnp.cos(ang), jnp.sin(ang)
    x1 = x[..., :half].astype(jnp.float32)
    x2 = x[..., half:].astype(jnp.float32)
    return jnp.concatenate([x1 * cos - x2 * sin, x1 * sin + x2 * cos], axis=-1).astype(x.dtype)


def sweep_query_blocks(fn, q_pos, q_arrays):
    b = q_arrays[0].shape[0]
    lq = q_pos.shape[0]
    qb = min(Q_BLOCK, lq)
    nb = -(-lq // qb)
    pad = nb * qb - lq

    def split(a):
        a = jnp.pad(a, [(0, 0), (0, pad)] + [(0, 0)] * (a.ndim - 2))
        return jnp.moveaxis(a.reshape((b, nb, qb) + a.shape[2:]), 1, 0)

    pos = jnp.pad(q_pos, (0, pad), mode="edge").reshape(nb, qb)
    out = lax.map(lambda args: fn(args[0], *args[1]), (pos, tuple(split(a) for a in q_arrays)))
    out = jnp.moveaxis(out, 0, 1)
    return out.reshape((b, nb * qb) + out.shape[3:])[:, :lq]


def stick_breaking_block(qpos, q, k, v, k_pos):
    z = jnp.einsum("bqhd,bkhd->bhqk", q, k, preferred_element_type=jnp.float32) / math.sqrt(SB_DH)
    mask = (k_pos[None, :] < qpos[:, None])[None, None]
    log_keep = jnp.where(mask, jax.nn.log_sigmoid(-z), 0.0)
    later = lax.cumsum(log_keep, axis=3, reverse=True) - log_keep
    a = jnp.where(mask, jnp.exp(jax.nn.log_sigmoid(z) + later), 0.0)
    return jnp.einsum("bhqk,bkhd->bqhd", a.astype(v.dtype), v)


def mla_block(qpos, q_lat, q_pe, c_kv, k_pe, k_pos):
    s = (jnp.einsum("bqhr,bkr->bhqk", q_lat, c_kv, preferred_element_type=jnp.float32)
         + jnp.einsum("bqhe,bke->bhqk", q_pe, k_pe, preferred_element_type=jnp.float32)) * MLA_SCALE
    mask = (k_pos[None, :] <= qpos[:, None])[None, None]
    p = jax.nn.softmax(jnp.where(mask, s, -jnp.inf), axis=-1)
    return jnp.einsum("bhqk,bkr->bqhr", p.astype(c_kv.dtype), c_kv)


def pool_mix(prev, a, pos, pool_w, pool_scale):
    b, L, _ = a.shape
    ext = jnp.concatenate([prev, a], axis=1)
    cs = jnp.pad(jnp.cumsum(ext.astype(jnp.float32), axis=1), ((0, 0), (1, 0), (0, 0)))
    end = cs[:, POOL_HIST + 1:]
    means = []
    for gi, w in enumerate(POOL_WINDOWS):
        sl = slice(gi * POOL_CG, (gi + 1) * POOL_CG)
        start = cs[:, POOL_HIST + 1 - w:POOL_HIST + 1 - w + L, sl]
        cnt = jnp.minimum(pos + 1, w).astype(jnp.float32)[None, :, None]
        means.append((end[..., sl] - start) / cnt)
    pooled = (jnp.concatenate(means, axis=-1) - a.astype(jnp.float32)).astype(a.dtype)
    y = jnp.einsum("blgc,gcd->blgd", pooled.reshape(b, L, POOL_G, POOL_CG), pool_w)
    return y.reshape(b, L, G_WIDTH) * pool_scale, ext[:, -POOL_HIST:]


def chunk_gate(u, v, ws, bs):
    b, L, H, d = v.shape
    cl = min(CHUNK, L)
    nc = -(-L // cl)
    pad = nc * cl - L
    vp = jnp.pad(v, ((0, 0), (0, pad), (0, 0), (0, 0))).reshape(b, nc, cl, H, d)
    w = ws[:, :cl, :cl] * jnp.tril(jnp.ones((cl, cl), ws.dtype))
    mixed = jnp.einsum("hrs,bnshd->bnrhd", w, vp) + bs[:, :cl].T[None, None, :, :, None]
    return u * mixed.reshape(b, nc * cl, H, d)[:, :L]


def trunk_layer(x, pos, k_pos, prev_pool, prev_conv, past, lw):
    (n1, w_in, pool_w, pool_scale, gln_g, gln_b, gws, gbs, qn, wuq, kvn, wukv,
     mixn, w_out, n2, w_up, cw, cb, w_down) = lw
    b, L, _ = x.shape
    h = rmsnorm(x, n1)
    z = h @ w_in
    offs = np.cumsum(IN_SIZES)[:-1].tolist()
    a_in, gu, gv, sq, sk, sv, cq, ckv, kr = jnp.split(z, offs, axis=-1)

    y_a, new_pool = pool_mix(prev_pool, a_in, pos, pool_w, pool_scale)

    u = jax.nn.gelu(gu)
    v_gm = layernorm(jax.nn.gelu(gv), gln_g, gln_b)
    y_b = chunk_gate(u.reshape(b, L, GM_HEADS, GM_DH), v_gm.reshape(b, L, GM_HEADS, GM_DH),
                     gws, gbs).reshape(b, L, G_WIDTH)

    q = sq.reshape(b, L, SB_HEADS, SB_DH)
    k = sk.reshape(b, L, SB_HEADS, SB_DH)
    vv = sv.reshape(b, L, SB_HEADS, SB_DH)
    qh = jnp.einsum("blr,rhe->blhe", rmsnorm(cq, qn), wuq)
    q_pe = rope(qh[..., MLA_NOPE:], pos)
    c_kv = rmsnorm(ckv, kvn)
    k_pe = rope(kr, pos)
    w_uk, w_uv = wukv[..., :MLA_NOPE], wukv[..., MLA_NOPE:]
    q_lat = jnp.einsum("blhd,rhd->blhr", qh[..., :MLA_NOPE], w_uk)

    if past is None:
        k_all, v_all, lat_all, pe_all = k, vv, c_kv, k_pe
    else:
        pk, pv, plat, ppe = past
        k_all = jnp.concatenate([pk, k], axis=1)
        v_all = jnp.concatenate([pv, vv], axis=1)
        lat_all = jnp.concatenate([plat, c_kv], axis=1)
        pe_all = jnp.concatenate([ppe, k_pe], axis=1)

    y_c = sweep_query_blocks(lambda qp, qq: stick_breaking_block(qp, qq, k_all, v_all, k_pos),
                             pos, (q,)).reshape(b, L, G_WIDTH)
    o_lat = sweep_query_blocks(lambda qp, ql, qr: mla_block(qp, ql, qr, lat_all, pe_all, k_pos),
                               pos, (q_lat, q_pe))
    y_d = jnp.einsum("blhr,rhv->blhv", o_lat, w_uv).reshape(b, L, G_WIDTH)

    mix = jnp.concatenate([y_a, y_b, y_c, y_d], axis=-1).reshape(b, L, N_GROUPS, G_WIDTH)
    mix = rmsnorm(mix, mixn.reshape(N_GROUPS, G_WIDTH)).reshape(b, L, D_MIX)
    x = x + mix @ w_out

    h2 = rmsnorm(x, n2)
    g, uf = jnp.split(h2 @ w_up, 2, axis=-1)
    g_ext = jnp.concatenate([prev_conv, g], axis=1)
    conv = cb
    for i in range(CONV_W):
        conv = conv + cw[i] * g_ext[:, i:i + L]
    x = x + (jax.nn.silu(conv) * uf) @ w_down
    new_conv = g_ext[:, -(CONV_W - 1):]
    return x, (k, vv, c_kv, k_pe, new_pool, new_conv, v_gm)


def setup_inputs(seed: int = 0) -> dict:
    key = jax.random.key(seed)
    ks = iter(jax.random.split(key, 40))

    def nrm(shape, scale):
        return jax.random.normal(next(ks), shape, jnp.float32) * scale

    def gain(shape):
        return 1.0 + nrm(shape, 0.02)

    n_pages = PAST_LEN // PAGE_SIZE
    n_used = DEC_BATCH * n_pages
    n_pool = n_used + (n_used + 3) // 4
    page_table = jax.random.permutation(next(ks), n_pool)[:n_used].reshape(DEC_BATCH, n_pages).astype(jnp.int32)

    return {
        "x_prompt": nrm((BATCH, SEQ, D_MODEL), 1.0),
        "x_sample": nrm((DEC_BATCH, DEC_SEQ, D_MODEL), 1.0),
        "cache_sb_k": nrm((DEPTH, n_pool, PAGE_SIZE, SB_HEADS, SB_DH), 1.0),
        "cache_sb_v": nrm((DEPTH, n_pool, PAGE_SIZE, SB_HEADS, SB_DH), 1.0),
        "cache_mla_latent": nrm((DEPTH, n_pool, PAGE_SIZE, MLA_KV_RANK), 1.0),
        "cache_mla_krope": nrm((DEPTH, n_pool, PAGE_SIZE, MLA_ROPE), 1.0),
        "state_pool": nrm((DEPTH, DEC_BATCH, POOL_HIST, G_WIDTH), 1.0),
        "state_ffn_conv": nrm((DEPTH, DEC_BATCH, CONV_W - 1, D_FF), 1.0),
        "page_table": page_table,
        "norm1": gain((DEPTH, D_MODEL)),
        "w_in": nrm((DEPTH, D_MODEL, D_IN), D_MODEL ** -0.5),
        "pool_w": nrm((DEPTH, POOL_G, POOL_CG, POOL_CG), POOL_CG ** -0.5),
        "pool_scale": 1.0 + nrm((DEPTH, G_WIDTH), 0.1),
        "gm_ln_g": gain((DEPTH, G_WIDTH)),
        "gm_ln_b": nrm((DEPTH, G_WIDTH), 0.02),
        "gm_ws": nrm((DEPTH, GM_HEADS, CHUNK, CHUNK), CHUNK ** -0.5),
        "gm_bs": 1.0 + nrm((DEPTH, GM_HEADS, CHUNK), 0.1),
        "mla_q_norm": gain((DEPTH, MLA_Q_RANK)),
        "mla_w_uq": nrm((DEPTH, MLA_Q_RANK, MLA_HEADS, MLA_NOPE + MLA_ROPE), MLA_Q_RANK ** -0.5),
        "mla_kv_norm": gain((DEPTH, MLA_KV_RANK)),
        "mla_w_ukv": nrm((DEPTH, MLA_KV_RANK, MLA_HEADS, MLA_NOPE + MLA_V), MLA_KV_RANK ** -0.5),
        "mix_norm": gain((DEPTH, D_MIX)),
        "w_out": nrm((DEPTH, D_MIX, D_MODEL), D_MIX ** -0.5),
        "norm2": gain((DEPTH, D_MODEL)),
        "w_up": nrm((DEPTH, D_MODEL, 2 * D_FF), D_MODEL ** -0.5),
        "conv_w": nrm((DEPTH, CONV_W, D_FF), CONV_W ** -0.5),
        "conv_b": nrm((DEPTH, D_FF), 0.02),
        "w_down": nrm((DEPTH, D_FF, D_MODEL), D_FF ** -0.5),
        "final_norm": gain((D_MODEL,)),
    }


def reference(x_prompt, x_sample, cache_sb_k, cache_sb_v, cache_mla_latent, cache_mla_krope,
              state_pool, state_ffn_conv, page_table, norm1, w_in, pool_w, pool_scale, gm_ln_g,
              gm_ln_b, gm_ws, gm_bs, mla_q_norm, mla_w_uq, mla_kv_norm, mla_w_ukv, mix_norm, w_out,
              norm2, w_up, conv_w, conv_b, w_down, final_norm):
    b_p, seq, _ = x_prompt.shape
    b_s, dec_seq, _ = x_sample.shape
    past_len = page_table.shape[1] * PAGE_SIZE
    pos_p = jnp.arange(seq, dtype=jnp.int32)
    pos_s = past_len + jnp.arange(dec_seq, dtype=jnp.int32)
    kpos_s = jnp.arange(past_len + dec_seq, dtype=jnp.int32)

    xp, xs = x_prompt, x_sample
    st_p, st_s = [], []
    for l in range(DEPTH):
        lw = (norm1[l], w_in[l], pool_w[l], pool_scale[l], gm_ln_g[l], gm_ln_b[l], gm_ws[l], gm_bs[l],
              mla_q_norm[l], mla_w_uq[l], mla_kv_norm[l], mla_w_ukv[l], mix_norm[l], w_out[l],
              norm2[l], w_up[l], conv_w[l], conv_b[l], w_down[l])
        xp, sp = trunk_layer(xp, pos_p, pos_p,
                             jnp.zeros((b_p, POOL_HIST, G_WIDTH), xp.dtype),
                             jnp.zeros((b_p, CONV_W - 1, D_FF), xp.dtype), None, lw)
        past = tuple(c[l, page_table].reshape((b_s, past_len) + c.shape[3:])
                     for c in (cache_sb_k, cache_sb_v, cache_mla_latent, cache_mla_krope))
        xs, ss = trunk_layer(xs, pos_s, kpos_s, state_pool[l], state_ffn_conv[l], past, lw)
        st_p.append(sp)
        st_s.append(ss)

    y_prompt = rmsnorm(xp, final_norm)
    y_sample = rmsnorm(xs, final_norm)
    new_sb_k_prompt = jnp.stack([s[0] for s in st_p])
    new_sb_v_prompt = jnp.stack([s[1] for s in st_p])
    new_latent_prompt = jnp.stack([s[2] for s in st_p])
    new_krope_prompt = jnp.stack([s[3] for s in st_p])
    new_pool_prompt = jnp.stack([s[4] for s in st_p])
    new_conv_prompt = jnp.stack([s[5] for s in st_p])
    new_sb_k_sample = jnp.stack([s[0] for s in st_s])
    new_sb_v_sample = jnp.stack([s[1] for s in st_s])
    new_latent_sample = jnp.stack([s[2] for s in st_s])
    new_krope_sample = jnp.stack([s[3] for s in st_s])
    new_pool_sample = jnp.stack([s[4] for s in st_s])
    new_conv_sample = jnp.stack([s[5] for s in st_s])
    new_gmlp_v_sample = jnp.stack([s[6] for s in st_s])
    return (y_prompt, y_sample, new_sb_k_prompt, new_sb_v_prompt, new_latent_prompt, new_krope_prompt,
            new_pool_prompt, new_conv_prompt, new_sb_k_sample, new_sb_v_sample, new_latent_sample,
            new_krope_sample, new_pool_sample, new_conv_sample, new_gmlp_v_sample)
```

```python
import functools
import math

import jax
import jax.numpy as jnp
from jax import lax
from jax.experimental import pallas as pl
from jax.experimental.pallas import tpu as pltpu

F32 = jnp.float32
BF16 = jnp.bfloat16

EPS = 1e-6
PAGE = 128
CHUNK = 128
N_HEADS = 4
G_WIDTH = 256
HEAD_DIM = 64
POOL_WINDOWS = (2, 4, 8, 16)
POOL_HIST = 15
MLA_NOPE = 64
MLA_ROPE = 32
MLA_RANK = 128
MLA_SCALE = 1.0 / math.sqrt(MLA_NOPE + MLA_ROPE)
ROPE_THETA = 10000.0
SB_SCALE = 0.125

VMEM_LIMIT_BYTES = 56 * 1024 * 1024

_NT = (((1,), (1,)), ((), ()))


def _cparams(*sem):
    return pltpu.CompilerParams(dimension_semantics=sem, vmem_limit_bytes=VMEM_LIMIT_BYTES)


def _dot(a, b):
    return jnp.dot(a, b, preferred_element_type=F32)


def _dot_nt(a, b):
    return lax.dot_general(a, b, _NT, preferred_element_type=F32)


def _rms(x):
    return x * lax.rsqrt(jnp.mean(x * x, axis=-1, keepdims=True) + EPS)


def _gelu(x):
    c = math.sqrt(2.0 / math.pi)
    return x * (0.5 * (1.0 + jnp.tanh(c * (x + 0.044715 * (x * x * x)))))


def _softplus(z):
    return jnp.maximum(z, 0.0) + jnp.log1p(jnp.exp(-jnp.abs(z)))


def _split_bf16(x, parts):
    out = []
    r = x
    for _ in range(parts):
        p = r.astype(BF16)
        out.append(p)
        r = r - p.astype(F32)
    return out


def _rope_lanes(x, cos, sin_signed):
    lane = lax.broadcasted_iota(jnp.int32, x.shape, 1)
    first_half = (lane % MLA_ROPE) < (MLA_ROPE // 2)
    partner = jnp.where(first_half, pltpu.roll(x, 128 - MLA_ROPE // 2, 1), pltpu.roll(x, MLA_ROPE // 2, 1))
    return x * cos + partner * sin_signed


def _inproj_kernel(x_ref, n1_ref, win_ref, lng_ref, lnb_ref, qn_ref, wqn_ref, wqp_ref, wuk_ref, kvn_ref,
                   cos_ref, sin_ref,
                   a_ref, u_ref, v_ref, sq_ref, sk_ref, sv_ref, qlat_ref, qpe_ref, ckv_ref, kpe_ref, kpet_ref):
    x = x_ref[...]
    hb = (_rms(x) * n1_ref[...]).astype(BF16)

    def seg(lo, hi):
        return _dot(hb, win_ref[:, lo:hi])

    a_ref[...] = seg(0, 256)
    u_ref[...] = _gelu(seg(256, 512))
    gv = _gelu(seg(512, 768))
    mu = jnp.mean(gv, axis=-1, keepdims=True)
    gc = gv - mu
    v_ref[...] = gc * lax.rsqrt(jnp.mean(gc * gc, axis=-1, keepdims=True) + EPS) * lng_ref[...] + lnb_ref[...]
    sq_ref[...] = seg(768, 1024)
    sk_ref[...] = seg(1024, 1280)
    sv_ref[...] = seg(1280, 1536)

    cos = cos_ref[...]
    sin = sin_ref[...]
    cqn = (_rms(seg(1536, 1792)) * qn_ref[...]).astype(BF16)
    q_nope = _dot(cqn, wqn_ref[...])
    qlat_ref[...] = _dot(q_nope.astype(BF16), wuk_ref[...])
    qpe_ref[...] = _rope_lanes(_dot(cqn, wqp_ref[...]), cos, sin)

    ckv_ref[...] = _rms(seg(1792, 1920)) * kvn_ref[...]
    kpe = _rope_lanes(seg(1920, 2048), cos, sin)
    kpe_ref[...] = kpe[:, :MLA_ROPE]
    kpet_ref[...] = kpe + pltpu.roll(kpe, 32, 1) + pltpu.roll(kpe, 64, 1) + pltpu.roll(kpe, 96, 1)


def _inproj(x, lw, cos_t, sin_t, tm):
    T, D = x.shape
    n_tab = cos_t.shape[0] // tm
    row = lambda i: (i, 0)
    const = lambda i: (0, 0)
    tab = lambda i: (i % n_tab, 0)
    widths = (256, 256, 256, 256, 256, 256, 512, 128, 128, MLA_ROPE, 128)
    full = lambda a: pl.BlockSpec(a.shape, const)
    weights = (lw["n1"], lw["w_in"], lw["ln_g"], lw["ln_b"], lw["qn"], lw["wq_nope"], lw["wq_pe"], lw["wuk_bd"],
               lw["kvn"])
    return pl.pallas_call(
        _inproj_kernel,
        grid=(T // tm,),
        in_specs=[pl.BlockSpec((tm, D), row)] + [full(w) for w in weights]
        + [pl.BlockSpec((tm, 128), tab), pl.BlockSpec((tm, 128), tab)],
        out_specs=[pl.BlockSpec((tm, w), row) for w in widths],
        out_shape=[jax.ShapeDtypeStruct((T, w), F32) for w in widths],
        compiler_params=_cparams("parallel"),
        name="inproj",
    )(x, *weights, cos_t, sin_t)


def _pool_select(wins, cnts, a):
    lane = lax.broadcasted_iota(jnp.int32, a.shape, 1)
    cg = G_WIDTH // len(POOL_WINDOWS)
    mean = wins[-1] / cnts[-1]
    for gi in range(len(POOL_WINDOWS) - 2, -1, -1):
        mean = jnp.where(lane < (gi + 1) * cg, wins[gi] / cnts[gi], mean)
    return mean - a


def _mixers_prompt_kernel(a_ref, halo_ref, u_ref, v_ref, pw_ref, ps_ref, gws_ref, gb_ref,
                          ya_ref, yb_ref, ext_ref, *, tm, tiles_per_seq):
    i = pl.program_id(0)
    t_in_seq = i % tiles_per_seq
    halo = jnp.where(t_in_seq == 0, 0.0, halo_ref[...])
    ext_ref[0:16, :] = halo
    ext_ref[16:, :] = a_ref[...]
    a = a_ref[...]
    run = a
    wins = []
    k = 1
    for w in POOL_WINDOWS:
        while k < w:
            run = run + ext_ref[16 - k:16 - k + tm, :]
            k += 1
        wins.append(run)
    pos = t_in_seq * tm + lax.broadcasted_iota(jnp.int32, (tm, 1), 0)
    cnts = [jnp.minimum(pos + 1, w).astype(F32) for w in POOL_WINDOWS]
    pooled = _pool_select(wins, cnts, a)
    ya_ref[...] = _dot(pooled.astype(BF16), pw_ref[...]) * ps_ref[...]

    r = lax.broadcasted_iota(jnp.int32, (CHUNK, CHUNK), 0)
    c = lax.broadcasted_iota(jnp.int32, (CHUNK, CHUNK), 1)
    tril = r >= c
    wts = [jnp.where(tril, gws_ref[h], 0.0).astype(BF16) for h in range(N_HEADS)]
    lane = lax.broadcasted_iota(jnp.int32, (CHUNK, G_WIDTH), 1)
    for ci in range(tm // CHUNK):
        sl = slice(ci * CHUNK, (ci + 1) * CHUNK)
        vc = v_ref[sl, :].astype(BF16)
        mixed = _dot(wts[N_HEADS - 1], vc)
        for h in range(N_HEADS - 2, -1, -1):
            mixed = jnp.where(lane < (h + 1) * HEAD_DIM, _dot(wts[h], vc), mixed)
        yb_ref[sl, :] = u_ref[sl, :] * (mixed + gb_ref[...])


def _mixers_prompt(a, u, v, lw, tm, seq):
    T = a.shape[0]
    tiles_per_seq = seq // tm
    row = lambda i: (i, 0)
    halo = lambda i: (jnp.maximum(i * (tm // 16) - 1, 0), 0)
    const2 = lambda i: (0, 0)
    const3 = lambda i: (0, 0, 0)
    return pl.pallas_call(
        functools.partial(_mixers_prompt_kernel, tm=tm, tiles_per_seq=tiles_per_seq),
        grid=(T // tm,),
        in_specs=[pl.BlockSpec((tm, G_WIDTH), row), pl.BlockSpec((16, G_WIDTH), halo),
                  pl.BlockSpec((tm, G_WIDTH), row), pl.BlockSpec((tm, G_WIDTH), row),
                  pl.BlockSpec((G_WIDTH, G_WIDTH), const2), pl.BlockSpec((1, G_WIDTH), const2),
                  pl.BlockSpec((N_HEADS, CHUNK, CHUNK), const3), pl.BlockSpec((CHUNK, G_WIDTH), const2)],
        out_specs=[pl.BlockSpec((tm, G_WIDTH), row), pl.BlockSpec((tm, G_WIDTH), row)],
        out_shape=[jax.ShapeDtypeStruct((T, G_WIDTH), F32)] * 2,
        scratch_shapes=[pltpu.VMEM((tm + 16, G_WIDTH), F32)],
        compiler_params=_cparams("parallel"),
        name="mixers_prompt",
    )(a, a, u, v, lw["pool_w_bd"], lw["pool_scale"], lw["gm_ws"], lw["gm_bias_full"])


def _mixers_sample_kernel(hist_ref, a_ref, u_ref, v_ref, pw_ref, ps_ref, wd_ref, b0_ref, ya_ref, yb_ref):
    a = a_ref[...]
    run = a
    wins = []
    k = 1
    for w in POOL_WINDOWS:
        while k < w:
            run = run + hist_ref[POOL_HIST - k]
            k += 1
        wins.append(run)
    cnts = [jnp.full((1, 1), float(w), F32) for w in POOL_WINDOWS]
    pooled = _pool_select(wins, cnts, a)
    ya_ref[...] = _dot(pooled.astype(BF16), pw_ref[...]) * ps_ref[...]
    yb_ref[...] = u_ref[...] * (v_ref[...] * wd_ref[...] + b0_ref[...])


def _mixers_sample(hist, a, u, v, lw):
    nb = a.shape[0]
    return pl.pallas_call(
        _mixers_sample_kernel,
        out_shape=[jax.ShapeDtypeStruct((nb, G_WIDTH), F32)] * 2,
        compiler_params=pltpu.CompilerParams(vmem_limit_bytes=VMEM_LIMIT_BYTES),
        name="mixers_sample",
    )(hist, a, u, v, lw["pool_w_bd"], lw["pool_scale"], lw["gm_w00"], lw["gm_b0"])


def _sb_block(qh, kb, vb, upper, carry, acc, mask):
    z = _dot_nt(qh, kb)
    lk = -_softplus(z)
    if mask is not None:
        lk = jnp.where(mask, lk, 0.0)
    hi, lo = _split_bf16(lk, 2)
    later = _dot(hi, upper) + _dot(lo, upper) + carry
    p = jnp.exp(z + lk + later)
    if mask is not None:
        p = jnp.where(mask, p, 0.0)
    acc = acc + _dot(p.astype(BF16), vb)
    carry = carry + jnp.sum(lk, axis=-1, keepdims=True)
    return carry, acc


def _sb_prompt_kernel(q_ref, k_ref, v_ref, o_ref, *, tq):
    i = pl.program_id(1)
    r = lax.broadcasted_iota(jnp.int32, (tq, tq), 0)
    c = lax.broadcasted_iota(jnp.int32, (tq, tq), 1)
    upper = jnp.where(r > c, 1.0, 0.0).astype(BF16)
    diag_mask = c < r
    for h in range(N_HEADS):
        hs = slice(h * HEAD_DIM, (h + 1) * HEAD_DIM)
        qh = (q_ref[:, hs] * SB_SCALE).astype(BF16)

        def kv(j):
            start = pl.multiple_of(j * tq, tq)
            return k_ref[pl.ds(start, tq), hs].astype(BF16), v_ref[pl.ds(start, tq), hs].astype(BF16)

        kb, vb = kv(i)
        carry, acc = _sb_block(qh, kb, vb, upper, jnp.zeros((tq, 1), F32), jnp.zeros((tq, HEAD_DIM), F32),
                               diag_mask)

        def body(jj, st):
            kb, vb = kv(i - 1 - jj)
            return _sb_block(qh, kb, vb, upper, st[0], st[1], None)

        carry, acc = lax.fori_loop(0, i, body, (carry, acc))
        o_ref[:, hs] = acc


def _sb_prompt(q, k, v, nbatch, seq, tq):
    T = q.shape[0]
    nq = seq // tq
    return pl.pallas_call(
        functools.partial(_sb_prompt_kernel, tq=tq),
        grid=(nbatch, nq),
        in_specs=[pl.BlockSpec((tq, G_WIDTH), lambda b, i: (b * nq + i, 0)),
                  pl.BlockSpec((seq, G_WIDTH), lambda b, i: (b, 0)),
                  pl.BlockSpec((seq, G_WIDTH), lambda b, i: (b, 0))],
        out_specs=pl.BlockSpec((tq, G_WIDTH), lambda b, i: (b * nq + i, 0)),
        out_shape=jax.ShapeDtypeStruct((T, G_WIDTH), F32),
        compiler_params=_cparams("parallel", "parallel"),
        name="sb_prompt",
    )(q, k, v)


def _mla_prompt_kernel(qlat_ref, qpe_ref, ckv_ref, kpet_ref, wuv_ref, o_ref, q_s, *, tq):
    i = pl.program_id(1)
    lane = lax.broadcasted_iota(jnp.int32, (tq, 128), 1)
    qpe = qpe_ref[...]
    for h in range(N_HEADS):
        q_s[h * tq:(h + 1) * tq, 0:128] = qlat_ref[:, h * 128:(h + 1) * 128].astype(BF16)
        q_s[h * tq:(h + 1) * tq, 128:256] = jnp.where(lane // MLA_ROPE == h, qpe, 0.0).astype(BF16)
    q = q_s[...]

    def block(j, m, l, acc, mask):
        start = pl.multiple_of(j * tq, tq)
        ckv = ckv_ref[pl.ds(start, tq), :].astype(BF16)
        kc = jnp.concatenate([ckv, kpet_ref[pl.ds(start, tq), :].astype(BF16)], axis=-1)
        s = _dot_nt(q, kc) * MLA_SCALE
        if mask is not None:
            s = jnp.where(mask, s, -jnp.inf)
        m_new = jnp.maximum(m, jnp.max(s, axis=-1, keepdims=True))
        alpha = jnp.exp(m - m_new)
        p = jnp.exp(s - m_new)
        l = l * alpha + jnp.sum(p, axis=-1, keepdims=True)
        acc = acc * alpha + _dot(p.astype(BF16), ckv)
        return m_new, l, acc

    m0 = jnp.full((N_HEADS * tq, 1), -jnp.inf, F32)
    l0 = jnp.zeros((N_HEADS * tq, 1), F32)
    a0 = jnp.zeros((N_HEADS * tq, MLA_RANK), F32)
    m, l, acc = lax.fori_loop(0, i, lambda j, st: block(j, st[0], st[1], st[2], None), (m0, l0, a0))
    r = lax.broadcasted_iota(jnp.int32, (N_HEADS * tq, tq), 0) % tq
    c = lax.broadcasted_iota(jnp.int32, (N_HEADS * tq, tq), 1)
    m, l, acc = block(i, m, l, acc, c <= r)
    o = acc / l
    o_cat = jnp.concatenate([o[h * tq:(h + 1) * tq] for h in range(N_HEADS)], axis=-1)
    o_ref[...] = _dot(o_cat.astype(BF16), wuv_ref[...])


def _mla_prompt(qlat, qpe, ckv, kpet, wuv_bd, nbatch, seq, tq):
    T = qlat.shape[0]
    nq = seq // tq
    qrow = lambda b, i: (b * nq + i, 0)
    kall = lambda b, i: (b, 0)
    return pl.pallas_call(
        functools.partial(_mla_prompt_kernel, tq=tq),
        grid=(nbatch, nq),
        in_specs=[pl.BlockSpec((tq, N_HEADS * MLA_RANK), qrow), pl.BlockSpec((tq, 128), qrow),
                  pl.BlockSpec((seq, MLA_RANK), kall), pl.BlockSpec((seq, 128), kall),
                  pl.BlockSpec(wuv_bd.shape, lambda b, i: (0, 0))],
        out_specs=pl.BlockSpec((tq, G_WIDTH), qrow),
        out_shape=jax.ShapeDtypeStruct((T, G_WIDTH), F32),
        scratch_shapes=[pltpu.VMEM((N_HEADS * tq, 256), BF16)],
        compiler_params=_cparams("parallel", "parallel"),
        name="mla_prompt",
    )(qlat, qpe, ckv, kpet, wuv_bd)


def _lane_sums_as_row(x):
    ones = jnp.ones((8, x.shape[1]), BF16)
    parts = _split_bf16(x, 3)
    tot = _dot_nt(ones, parts[0]) + _dot_nt(ones, parts[1]) + _dot_nt(ones, parts[2])
    return tot[0:1]


def _head_rows_to_row(x):
    lane = lax.broadcasted_iota(jnp.int32, (1, G_WIDTH), 1)
    out = x[N_HEADS - 1:N_HEADS]
    for h in range(N_HEADS - 2, -1, -1):
        out = jnp.where(lane < (h + 1) * HEAD_DIM, x[h:h + 1], out)
    return out


def _decode_kernel(pt_ref, *refs, G):
    del pt_ref
    kt, vt, lat, krt = refs[0:G], refs[G:2 * G], refs[2 * G:3 * G], refs[3 * G:4 * G]
    sq_ref, qlat_ref, qpe_ref, ckvn_ref, kpen_ref, wuv_ref, yc_ref, yd_ref = refs[4 * G:4 * G + 8]
    qbd_s, ql_s, qp_s, accv_s, carry_s, m_s, l_s, acco_s = refs[4 * G + 8:]
    p = pl.program_id(1)

    @pl.when(p == 0)
    def _init():
        row = lax.broadcasted_iota(jnp.int32, (8, G_WIDTH), 0)
        lane = lax.broadcasted_iota(jnp.int32, (8, G_WIDTH), 1)
        q = jnp.broadcast_to(sq_ref[0] * SB_SCALE, (8, G_WIDTH))
        qbd_s[...] = jnp.where(lane // HEAD_DIM == row, q, 0.0)
        accv_s[...] = jnp.zeros_like(accv_s)
        carry_s[...] = jnp.zeros_like(carry_s)
        ql_s[...] = jnp.zeros_like(ql_s)
        qp_s[...] = jnp.zeros_like(qp_s)
        ql_s[0:N_HEADS, :] = qlat_ref[0]
        qp_s[0:N_HEADS, :] = qpe_ref[0]
        ql = ql_s[...]
        qp = qp_s[...]
        s_self = (jnp.sum(ql * ckvn_ref[0], axis=-1, keepdims=True)
                  + jnp.sum(qp * kpen_ref[0], axis=-1, keepdims=True)) * MLA_SCALE
        m_s[...] = s_self
        l_s[...] = jnp.ones_like(l_s)
        acco_s[...] = jnp.broadcast_to(ckvn_ref[0], (8, MLA_RANK))

    qbd = qbd_s[...]
    z = jnp.concatenate([_dot(qbd, kt[g][...]) for g in range(G)], axis=0)
    lk = -_softplus(z)
    r = lax.broadcasted_iota(jnp.int32, (PAGE, PAGE), 0)
    c = lax.broadcasted_iota(jnp.int32, (PAGE, PAGE), 1)
    upper = jnp.where(r > c, 1.0, 0.0).astype(BF16)
    hi, lo = _split_bf16(lk, 2)
    later = _dot(hi, upper) + _dot(lo, upper)
    tot = jnp.sum(lk, axis=-1, keepdims=True)
    carry = carry_s[...]
    carries = []
    for g in range(G):
        carries.append(carry)
        carry = carry + tot[8 * g:8 * g + 8]
    carry_s[...] = carry
    a = jnp.exp(z + lk + later + jnp.concatenate(carries, axis=0))
    accv = accv_s[...]
    for g in range(G):
        a_exp = jnp.concatenate(
            [jnp.broadcast_to(a[8 * g + h:8 * g + h + 1], (HEAD_DIM, PAGE)) for h in range(N_HEADS)], axis=0)
        accv = accv + vt[g][...] * a_exp
    accv_s[...] = accv

    ql = ql_s[...]
    qp = qp_s[...]
    s = jnp.concatenate([_dot_nt(ql, lat[g][...]) + _dot(qp, krt[g][...]) for g in range(G)], axis=0) * MLA_SCALE
    smax = s[0:8]
    for g in range(1, G):
        smax = jnp.maximum(smax, s[8 * g:8 * g + 8])
    m_old = m_s[...]
    m_new = jnp.maximum(m_old, jnp.max(smax, axis=-1, keepdims=True))
    alpha = jnp.exp(m_old - m_new)
    pr = jnp.exp(s - jnp.concatenate([m_new] * G, axis=0))
    psum = pr[0:8]
    pv = _dot(pr[0:8], lat[0][...])
    for g in range(1, G):
        psum = psum + pr[8 * g:8 * g + 8]
        pv = pv + _dot(pr[8 * g:8 * g + 8], lat[g][...])
    m_s[...] = m_new
    l_s[...] = l_s[...] * alpha + jnp.sum(psum, axis=-1, keepdims=True)
    acco_s[...] = acco_s[...] * alpha + pv

    @pl.when(p == pl.num_programs(1) - 1)
    def _finish():
        yc_ref[0] = _lane_sums_as_row(accv_s[...])
        o = acco_s[...] / l_s[...]
        yd_ref[0] = _head_rows_to_row(_dot(o, wuv_ref[...]))


def _decode_attention(layer, page_table_flat, n_pages, kt, vt, lat, krt, sq, qlat, qpe, ckv_new, kpe_new, wuv_all,
                      pages_per_step):
    nb = sq.shape[0]
    G = pages_per_step
    steps = n_pages // G

    def page_map(g):
        return lambda b, p, pt: (layer, pt[b * n_pages + (n_pages - 1 - (p * G + g))], 0, 0)

    def paged(arr):
        return [pl.BlockSpec((None, None) + arr.shape[2:], page_map(g)) for g in range(G)]

    per_seq = lambda b, p, pt: (b, 0, 0)
    in_specs = (paged(kt) + paged(vt) + paged(lat) + paged(krt)
                + [pl.BlockSpec((1, 1, G_WIDTH), per_seq), pl.BlockSpec((1, N_HEADS, MLA_RANK), per_seq),
                   pl.BlockSpec((1, N_HEADS, MLA_ROPE), per_seq), pl.BlockSpec((1, 1, MLA_RANK), per_seq),
                   pl.BlockSpec((1, 1, MLA_ROPE), per_seq),
                   pl.BlockSpec(wuv_all.shape, lambda b, p, pt: (0, 0))])
    out_specs = [pl.BlockSpec((1, 1, G_WIDTH), per_seq), pl.BlockSpec((1, 1, G_WIDTH), per_seq)]
    scratch = [pltpu.VMEM((8, G_WIDTH), F32), pltpu.VMEM((8, MLA_RANK), F32), pltpu.VMEM((8, MLA_ROPE), F32),
               pltpu.VMEM((G_WIDTH, PAGE), F32), pltpu.VMEM((8, 1), F32), pltpu.VMEM((8, 1), F32),
               pltpu.VMEM((8, 1), F32), pltpu.VMEM((8, MLA_RANK), F32)]
    yc, yd = pl.pallas_call(
        functools.partial(_decode_kernel, G=G),
        grid_spec=pltpu.PrefetchScalarGridSpec(
            num_scalar_prefetch=1, grid=(nb, steps), in_specs=in_specs, out_specs=out_specs,
            scratch_shapes=scratch),
        out_shape=[jax.ShapeDtypeStruct((nb, 1, G_WIDTH), F32)] * 2,
        compiler_params=_cparams("parallel", "arbitrary"),
        name="decode_attention",
    )(page_table_flat, *([kt] * G), *([vt] * G), *([lat] * G), *([krt] * G),
      sq.reshape(nb, 1, G_WIDTH), qlat.reshape(nb, N_HEADS, MLA_RANK), qpe.reshape(nb, N_HEADS, MLA_ROPE),
      ckv_new.reshape(nb, 1, MLA_RANK), kpe_new.reshape(nb, 1, MLA_ROPE), wuv_all)
    return yc.reshape(nb, G_WIDTH), yd.reshape(nb, G_WIDTH)


def _merge_ffn_kernel(*refs, tm, d_ff, fc, seq_mode, tiles_per_seq, final):
    (x_ref, ya_ref, yb_ref, yc_ref, yd_ref, mixn_ref, wout_ref, n2_ref, wup_ref, cw_ref, cb_ref, wdn_ref,
     fn_ref) = refs[:13]
    if seq_mode:
        xo_ref, gt_ref, gbuf_s, gcarry_s = refs[13:]
    else:
        gm2_ref, gm1_ref, xo_ref, g_ref = refs[13:]
    mixn = mixn_ref[...]
    parts = []
    for gi, y_ref in enumerate((ya_ref, yb_ref, yc_ref, yd_ref)):
        parts.append((_rms(y_ref[...]) * mixn[:, gi * G_WIDTH:(gi + 1) * G_WIDTH]).astype(BF16))
    mix = jnp.concatenate(parts, axis=-1)
    x1 = x_ref[...] + _dot(mix, wout_ref[...])
    h2 = (_rms(x1) * n2_ref[...]).astype(BF16)

    if seq_mode:
        @pl.when(pl.program_id(0) % tiles_per_seq == 0)
        def _zero_history():
            gcarry_s[...] = jnp.zeros_like(gcarry_s)

    acc = x1
    for ci in range(d_ff // fc):
        cs = slice(ci * fc, (ci + 1) * fc)
        g = _dot(h2, wup_ref[:, ci * fc:(ci + 1) * fc])
        uf = _dot(h2, wup_ref[:, d_ff + ci * fc:d_ff + (ci + 1) * fc])
        if seq_mode:
            gbuf_s[0:8, :] = gcarry_s[:, cs]
            gbuf_s[8:, :] = g
            gm1 = gbuf_s[7:7 + tm, :]
            gm2 = gbuf_s[6:6 + tm, :]
            gcarry_s[:, cs] = g[tm - 8:tm]
        else:
            gm1 = gm1_ref[:, cs]
            gm2 = gm2_ref[:, cs]
            g_ref[:, cs] = g
        conv = cb_ref[:, cs] + cw_ref[0:1, cs] * gm2 + cw_ref[1:2, cs] * gm1 + cw_ref[2:3, cs] * g
        act = conv * (1.0 / (1.0 + jnp.exp(-conv))) * uf
        acc = acc + _dot(act.astype(BF16), wdn_ref[cs, :])
    if seq_mode:
        gt_ref[0] = gcarry_s[...]
    if final:
        acc = _rms(acc) * fn_ref[...]
    xo_ref[...] = acc


def _merge_ffn(x, ys, lw, final_norm, tm, fc, seq, prev=None, final=False):
    T, D = x.shape
    d_ff = lw["w_down"].shape[0]
    seq_mode = prev is None
    tiles_per_seq = (seq // tm) if seq_mode else 1
    row = lambda i: (i, 0)
    const = lambda i: (0, 0)
    single = dict(pipeline_mode=pl.Buffered(1))
    wspec = lambda a: pl.BlockSpec(a.shape, const, **single)
    weights = (lw["mixn"], lw["w_out"], lw["n2"], lw["w_up"], lw["conv_w"], lw["conv_b"], lw["w_down"], final_norm)
    in_specs = ([pl.BlockSpec((tm, D), row)] + [pl.BlockSpec((tm, G_WIDTH), row)] * 4 + [wspec(w) for w in weights])
    args = [x, *ys, *weights]
    kern = functools.partial(_merge_ffn_kernel, tm=tm, d_ff=d_ff, fc=fc, seq_mode=seq_mode,
                             tiles_per_seq=tiles_per_seq, final=final)
    if seq_mode:
        nseq = T // seq
        out_specs = [pl.BlockSpec((tm, D), row), pl.BlockSpec((1, 8, d_ff), lambda i: (i // tiles_per_seq, 0, 0))]
        out_shape = [jax.ShapeDtypeStruct((T, D), F32), jax.ShapeDtypeStruct((nseq, 8, d_ff), F32)]
        scratch = [pltpu.VMEM((tm + 8, fc), F32), pltpu.VMEM((8, d_ff), F32)]
    else:
        in_specs += [pl.BlockSpec((tm, d_ff), row)] * 2
        args += [prev[0], prev[1]]
        out_specs = [pl.BlockSpec((tm, D), row), pl.BlockSpec((tm, d_ff), row)]
        out_shape = [jax.ShapeDtypeStruct((T, D), F32), jax.ShapeDtypeStruct((T, d_ff), F32)]
        scratch = []
    return pl.pallas_call(
        kern, grid=(T // tm,), in_specs=in_specs, out_specs=out_specs, out_shape=out_shape,
        scratch_shapes=scratch, compiler_params=_cparams("arbitrary"), name="merge_ffn",
    )(*args)


def _block_diag(blocks):
    n = len(blocks)
    rows = []
    for i, blk in enumerate(blocks):
        rows.append(jnp.concatenate(
            [blk if j == i else jnp.zeros((blk.shape[0], blocks[j].shape[1]), blk.dtype) for j in range(n)], axis=1))
    return jnp.concatenate(rows, axis=0)


def _prep_layer(l, norm1, w_in, pool_w, pool_scale, gm_ln_g, gm_ln_b, gm_ws, gm_bs, mla_q_norm, mla_w_uq,
                mla_kv_norm, mla_w_ukv, mix_norm, w_out, norm2, w_up, conv_w, conv_b, w_down):
    d_in = w_in.shape[2]
    wuq = mla_w_uq[l]
    wukv = mla_w_ukv[l]
    rq = wuq.shape[0]
    w_uk = wukv[:, :, :MLA_NOPE]
    w_uv = wukv[:, :, MLA_NOPE:]
    return {
        "n1": norm1[l][None], "w_in": jnp.pad(w_in[l], ((0, 0), (0, 2048 - d_in))).astype(BF16),
        "ln_g": gm_ln_g[l][None], "ln_b": gm_ln_b[l][None], "qn": mla_q_norm[l][None],
        "wq_nope": wuq[:, :, :MLA_NOPE].reshape(rq, N_HEADS * MLA_NOPE).astype(BF16),
        "wq_pe": wuq[:, :, MLA_NOPE:].reshape(rq, N_HEADS * MLA_ROPE).astype(BF16),
        "wuk_bd": _block_diag([w_uk[:, h, :].T for h in range(N_HEADS)]).astype(BF16),
        "wuv_bd": _block_diag([w_uv[:, h, :] for h in range(N_HEADS)]).astype(BF16),
        "wuv_all": w_uv.reshape(MLA_RANK, G_WIDTH),
        "kvn": mla_kv_norm[l][None],
        "pool_w_bd": _block_diag([pool_w[l, g] for g in range(len(POOL_WINDOWS))]).astype(BF16),
        "pool_scale": pool_scale[l][None],
        "gm_ws": gm_ws[l],
        "gm_bias_full": jnp.repeat(gm_bs[l].T, HEAD_DIM, axis=1),
        "gm_w00": jnp.repeat(gm_ws[l][:, 0, 0], HEAD_DIM)[None],
        "gm_b0": jnp.repeat(gm_bs[l][:, 0], HEAD_DIM)[None],
        "mixn": mix_norm[l][None], "w_out": w_out[l].astype(BF16), "n2": norm2[l][None],
        "w_up": w_up[l].astype(BF16), "conv_w": conv_w[l], "conv_b": conv_b[l][None],
        "w_down": w_down[l].astype(BF16),
    }


def _rope_tables(pos):
    half = MLA_ROPE // 2
    freq = ROPE_THETA ** (-jnp.arange(half, dtype=F32) / half)
    ang = pos.astype(F32)[:, None] * freq
    cos, sin = jnp.cos(ang), jnp.sin(ang)
    reps = 128 // MLA_ROPE
    return jnp.tile(jnp.concatenate([cos, cos], axis=1), (1, reps)), jnp.tile(
        jnp.concatenate([-sin, sin], axis=1), (1, reps))


TM_PROMPT = 512
TQ_PROMPT = 256
FF_CHUNK = 256
PAGES_PER_STEP = 8


def kernel(x_prompt, x_sample, cache_sb_k, cache_sb_v, cache_mla_latent, cache_mla_krope, state_pool, state_ffn_conv, page_table, norm1, w_in, pool_w, pool_scale, gm_ln_g, gm_ln_b, gm_ws, gm_bs, mla_q_norm, mla_w_uq, mla_kv_norm, mla_w_ukv, mix_norm, w_out, norm2, w_up, conv_w, conv_b, w_down, final_norm):
    nb_p, seq, d_model = x_prompt.shape
    nb_s, dec_seq, _ = x_sample.shape
    assert dec_seq == 1
    depth = norm1.shape[0]
    n_pages = page_table.shape[1]
    n_pool = cache_sb_k.shape[1]
    past_len = n_pages * PAGE
    assert past_len >= max(POOL_WINDOWS)
    tm_p = min(TM_PROMPT, seq)
    tq = min(TQ_PROMPT, seq)
    G = min(PAGES_PER_STEP, n_pages)
    assert seq % tm_p == 0 and seq % tq == 0 and n_pages % G == 0 and tm_p % CHUNK == 0

    cos_p, sin_p = _rope_tables(jnp.arange(seq, dtype=jnp.int32))
    cos_s, sin_s = _rope_tables(jnp.full((nb_s,), past_len, jnp.int32))
    fn = final_norm[None]

    kt = jnp.transpose(cache_sb_k, (0, 1, 3, 4, 2)).reshape(depth, n_pool, G_WIDTH, PAGE)
    vt = jnp.transpose(cache_sb_v, (0, 1, 3, 4, 2)).reshape(depth, n_pool, G_WIDTH, PAGE)
    krt = jnp.transpose(cache_mla_krope, (0, 1, 3, 2))
    pt_flat = page_table.reshape(-1)

    xp = x_prompt.reshape(nb_p * seq, d_model)
    xs = x_sample.reshape(nb_s, d_model)
    st_p, st_s = [], []
    for l in range(depth):
        lw = _prep_layer(l, norm1, w_in, pool_w, pool_scale, gm_ln_g, gm_ln_b, gm_ws, gm_bs, mla_q_norm, mla_w_uq,
                         mla_kv_norm, mla_w_ukv, mix_norm, w_out, norm2, w_up, conv_w, conv_b, w_down)
        last = l == depth - 1

        a, u, v, sq, sk, sv, qlat, qpe, ckv, kpe, kpet = _inproj(xp, lw, cos_p, sin_p, tm_p)
        ya, yb = _mixers_prompt(a, u, v, lw, tm_p, seq)
        yc = _sb_prompt(sq, sk, sv, nb_p, seq, tq)
        yd = _mla_prompt(qlat, qpe, ckv, kpet, lw["wuv_bd"], nb_p, seq, tq)
        xp, g_tail = _merge_ffn(xp, (ya, yb, yc, yd), lw, fn, tm_p, FF_CHUNK, seq, final=last)
        st_p.append((sk, sv, ckv, kpe, a.reshape(nb_p, seq, G_WIDTH)[:, seq - POOL_HIST:], g_tail[:, 6:8]))

        a, u, v, sq, sk, sv, qlat, qpe, ckv, kpe, _ = _inproj(xs, lw, cos_s, sin_s, nb_s)
        hist = jnp.transpose(state_pool[l], (1, 0, 2))
        ya, yb = _mixers_sample(hist, a, u, v, lw)
        yc, yd = _decode_attention(l, pt_flat, n_pages, kt, vt, cache_mla_latent, krt, sq, qlat, qpe, ckv, kpe,
                                   lw["wuv_all"], G)
        prev = state_ffn_conv[l]
        xs, g_new = _merge_ffn(xs, (ya, yb, yc, yd), lw, fn, nb_s, FF_CHUNK, 1, prev=(prev[:, 0], prev[:, 1]),
                               final=last)
        st_s.append((sk, sv, ckv, kpe,
                     jnp.concatenate([state_pool[l][:, 1:], a[:, None]], axis=1),
                     jnp.concatenate([prev[:, 1:], g_new[:, None]], axis=1), v))

    def stack(items, shape):
        return jnp.stack(items).reshape((depth,) + shape)

    hp = (nb_p, seq, N_HEADS, HEAD_DIM)
    hs = (nb_s, 1, N_HEADS, HEAD_DIM)
    return (xp.reshape(nb_p, seq, d_model), xs.reshape(nb_s, 1, d_model),
            stack([s[0] for s in st_p], hp), stack([s[1] for s in st_p], hp),
            stack([s[2] for s in st_p], (nb_p, seq, MLA_RANK)), stack([s[3] for s in st_p], (nb_p, seq, MLA_ROPE)),
            jnp.stack([s[4] for s in st_p]), jnp.stack([s[5] for s in st_p]),
            stack([s[0] for s in st_s], hs), stack([s[1] for s in st_s], hs),
            stack([s[2] for s in st_s], (nb_s, 1, MLA_RANK)), stack([s[3] for s in st_s], (nb_s, 1, MLA_ROPE)),
            jnp.stack([s[4] for s in st_s]), jnp.stack([s[5] for s in st_s]),
            stack([s[6] for s in st_s], (nb_s, 1, G_WIDTH)))
```
